```python
import jax, jax.numpy as jnp
from jax import lax
import numpy as np

D_MODEL = 1024
BATCH = 2
SEQ = 8192
DEPTH = 4

N_MIXERS = 2
N_GLA_LAYERS = (DEPTH + N_MIXERS - 1) // N_MIXERS
N_CONV_LAYERS = DEPTH // N_MIXERS
GLA_HEADS = 4
GLA_KEY_DIM = D_MODEL // 2
GLA_VAL_DIM = D_MODEL
GLA_DK = GLA_KEY_DIM // GLA_HEADS
GLA_DV = GLA_VAL_DIM // GLA_HEADS
GLA_GATE_RANK = 16
GLA_GATE_NORM = 16.0
GLA_CHUNK = 64
GLA_IN = 2 * GLA_KEY_DIM + 2 * GLA_VAL_DIM + GLA_GATE_RANK
CONV_WIDTH = 31
D_FF = 2816
FFN_CONV_WIDTH = 3
EPS = 1e-6

kernel_name = "hybrid_gla_conformer_convffn"


def rms_norm(x, g):
    xf = x.astype(jnp.float32)
    y = xf * lax.rsqrt(jnp.mean(xf * xf, axis=-1, keepdims=True) + EPS)
    return (y * g.astype(jnp.float32)).astype(x.dtype)


def layer_norm(x, g, b):
    xf = x.astype(jnp.float32)
    mu = jnp.mean(xf, axis=-1, keepdims=True)
    xc = xf - mu
    y = xc * lax.rsqrt(jnp.mean(xc * xc, axis=-1, keepdims=True) + EPS)
    return (y * g.astype(jnp.float32) + b.astype(jnp.float32)).astype(x.dtype)


def causal_dwconv(x, w):
    k_width, c = w.shape
    return lax.conv_general_dilated(
        x, w[:, None, :].astype(x.dtype), window_strides=(1,),
        padding=[(k_width - 1, 0)], dimension_numbers=("NWC", "WIO", "NWC"),
        feature_group_count=c)


def gla_mix(h, w_in, w_gk2, b_gk, g_norm, w_o):
    bsz, t, _ = h.shape
    n_chunks = t // GLA_CHUNK
    proj = h @ w_in
    s1, s2 = GLA_KEY_DIM, 2 * GLA_KEY_DIM
    s3, s4 = s2 + GLA_VAL_DIM, s2 + 2 * GLA_VAL_DIM
    q, k, v = proj[..., :s1], proj[..., s1:s2], proj[..., s2:s3]
    og, r = proj[..., s3:s4], proj[..., s4:]
    lg = jax.nn.log_sigmoid((r @ w_gk2 + b_gk).astype(jnp.float32)) / GLA_GATE_NORM

    def to_chunks(z, d):
        z = z.astype(jnp.float32).reshape(bsz, n_chunks, GLA_CHUNK, GLA_HEADS, d)
        return z.transpose(0, 3, 1, 2, 4)

    q = to_chunks(q, GLA_DK) * (GLA_DK ** -0.5)
    k = to_chunks(k, GLA_DK)
    v = to_chunks(v, GLA_DV)
    b = jnp.cumsum(to_chunks(lg, GLA_DK), axis=3)
    b_last = b[:, :, :, -1:, :]
    qe = q * jnp.exp(b)
    ke = k * jnp.exp(-b)
    kd = k * jnp.exp(b_last - b)
    causal = jnp.tril(jnp.ones((GLA_CHUNK, GLA_CHUNK), dtype=bool))
    att = jnp.einsum('bhncd,bhnsd->bhncs', qe, ke)
    att = jnp.where(causal, att, 0.0)
    o_intra = jnp.einsum('bhncs,bhnsv->bhncv', att, v)
    u = jnp.einsum('bhncd,bhncv->bhndv', kd, v)
    decay = jnp.exp(b_last[:, :, :, 0, :])

    def step(state, inp):
        d, un = inp
        return d[..., None] * state + un, state

    s0 = jnp.zeros((bsz, GLA_HEADS, GLA_DK, GLA_DV), jnp.float32)
    _, s_prev = lax.scan(step, s0, (jnp.moveaxis(decay, 2, 0), jnp.moveaxis(u, 2, 0)))
    o_inter = jnp.einsum('bhncd,nbhdv->bhncv', qe, s_prev)
    o = (o_intra + o_inter).transpose(0, 2, 3, 1, 4).reshape(bsz, t, GLA_HEADS, GLA_DV)
    o = o * lax.rsqrt(jnp.mean(o * o, axis=-1, keepdims=True) + EPS) * g_norm.astype(jnp.float32)
    o = o * jax.nn.silu(og.astype(jnp.float32).reshape(bsz, t, GLA_HEADS, GLA_DV))
    return o.reshape(bsz, t, GLA_VAL_DIM).astype(h.dtype) @ w_o


def conformer_conv_mix(h, w_pw1, w_dw, b_dw, ln_g, ln_b, w_pw2):
    a, gate = jnp.split(h @ w_pw1, 2, axis=-1)
    z = a * jax.nn.sigmoid(gate)
    z = causal_dwconv(z, w_dw) + b_dw
    z = jax.nn.silu(layer_norm(z, ln_g, ln_b))
    return z @ w_pw2


def conv_ffn(h, w_up, w_dw, b_dw, w_down):
    z = causal_dwconv(h @ w_up, w_dw) + b_dw
    val, gate = jnp.split(z, 2, axis=-1)
    return (jax.nn.silu(gate) * val) @ w_down


def setup_inputs(seed: int = 0) -> dict:
    key = jax.random.key(seed)
    ks = jax.random.split(key, 24)
    f32 = jnp.float32
    nrm = lambda k, shape, scale: jax.random.normal(k, shape, f32) * scale
    na, nc, L, D = N_GLA_LAYERS, N_CONV_LAYERS, DEPTH, D_MODEL
    return {
        "x": nrm(ks[0], (BATCH, SEQ, D), 1.0),
        "gla_w_in": nrm(ks[1], (na, D, GLA_IN), D ** -0.5),
        "gla_w_gk2": nrm(ks[2], (na, GLA_GATE_RANK, GLA_KEY_DIM), GLA_GATE_RANK ** -0.5),
        "gla_b_gk": nrm(ks[3], (na, GLA_KEY_DIM), 0.01),
        "gla_norm": 1.0 + nrm(ks[4], (na, GLA_DV), 0.01),
        "gla_w_o": nrm(ks[5], (na, GLA_VAL_DIM, D), GLA_VAL_DIM ** -0.5),
        "cm_w_pw1": nrm(ks[6], (nc, D, 2 * D), D ** -0.5),
        "cm_w_dw": nrm(ks[7], (nc, CONV_WIDTH, D), CONV_WIDTH ** -0.5),
        "cm_b_dw": nrm(ks[8], (nc, D), 0.01),
        "cm_ln_g": 1.0 + nrm(ks[9], (nc, D), 0.01),
        "cm_ln_b": nrm(ks[10], (nc, D), 0.01),
        "cm_w_pw2": nrm(ks[11], (nc, D, D), D ** -0.5),
        "ffn_w_up": nrm(ks[12], (L, D, 2 * D_FF), D ** -0.5),
        "ffn_w_dw": nrm(ks[13], (L, FFN_CONV_WIDTH, 2 * D_FF), FFN_CONV_WIDTH ** -0.5),
        "ffn_b_dw": nrm(ks[14], (L, 2 * D_FF), 0.01),
        "ffn_w_down": nrm(ks[15], (L, D_FF, D), D_FF ** -0.5),
        "norm_mix": 1.0 + nrm(ks[16], (L, D), 0.01),
        "norm_ffn": 1.0 + nrm(ks[17], (L, D), 0.01),
        "norm_final": 1.0 + nrm(ks[18], (D,), 0.01),
    }


def reference(x, gla_w_in, gla_w_gk2, gla_b_gk, gla_norm, gla_w_o,
              cm_w_pw1, cm_w_dw, cm_b_dw, cm_ln_g, cm_ln_b, cm_w_pw2,
              ffn_w_up, ffn_w_dw, ffn_b_dw, ffn_w_down,
              norm_mix, norm_ffn, norm_final):
    h = x
    for i in range(DEPTH):
        j = i // N_MIXERS
        hn = rms_norm(h, norm_mix[i])
        if i % N_MIXERS == 0:
            mix = gla_mix(hn, gla_w_in[j], gla_w_gk2[j], gla_b_gk[j], gla_norm[j], gla_w_o[j])
        else:
            mix = conformer_conv_mix(hn, cm_w_pw1[j], cm_w_dw[j], cm_b_dw[j],
                                     cm_ln_g[j], cm_ln_b[j], cm_w_pw2[j])
        h = h + mix
        h = h + conv_ffn(rms_norm(h, norm_ffn[i]), ffn_w_up[i], ffn_w_dw[i],
                         ffn_b_dw[i], ffn_w_down[i])
    return rms_norm(h, norm_final)
```

```python
import functools

import jax
import jax.numpy as jnp
from jax import lax
from jax.experimental import pallas as pl
from jax.experimental.pallas import tpu as pltpu

F32 = jnp.float32
BF16 = jnp.bfloat16
EPS = 1e-6

V7X_SUBLANES = 8
V7X_LANES = 128
V7X_VMEM_LIMIT_BYTES = 56 * 1024 * 1024

GLA_HEADS = 4
GLA_CHUNK = 64
GLA_GATE_NORM = 16.0
GLA_RANK_PAD = V7X_LANES

CONF_HALO = 32


def _rms(x, g):
    return x * lax.rsqrt(jnp.mean(x * x, axis=-1, keepdims=True) + EPS) * g


def _dot(a, b):
    return jnp.dot(a, b, preferred_element_type=F32)


def _dot_nt(a, b):
    return lax.dot_general(a, b, (((1,), (1,)), ((), ())), preferred_element_type=F32)


def _ffn_kernel(h_ref, g_ref, wup_ref, wdw_ref, bdw_ref, wdown_ref, gfin_ref,
                o_ref, act_ref, carry_ref, *, d_ff, cn, final_norm):
    tm = h_ref.shape[0]

    @pl.when(pl.program_id(1) == 0)
    def _():
        carry_ref[...] = jnp.zeros_like(carry_ref)

    h = h_ref[...]
    hn = _rms(h, g_ref[...]).astype(BF16)
    rows8 = lax.broadcasted_iota(jnp.int32, (V7X_SUBLANES, cn), 0)

    def conv(col0, idx):
        z = _dot(hn, wup_ref[:, col0:col0 + cn])
        prev = carry_ref[idx]
        carry_ref[idx] = z[tm - V7X_SUBLANES:tm]
        w = wdw_ref[:, col0:col0 + cn]
        y = z * w[2:3] + bdw_ref[:, col0:col0 + cn]
        for s in (1, 2):
            zs = pltpu.roll(z, s, 0)
            top = jnp.where(rows8 < s, pltpu.roll(prev, s, 0), zs[:V7X_SUBLANES])
            zs = jnp.concatenate([top, zs[V7X_SUBLANES:]], axis=0)
            y = y + zs * w[2 - s:3 - s]
        return y

    for c in range(d_ff // cn):
        val = conv(c * cn, 2 * c)
        gate = conv(d_ff + c * cn, 2 * c + 1)
        act_ref[:, c * cn:(c + 1) * cn] = (jax.nn.silu(gate) * val).astype(BF16)

    out = h + _dot(act_ref[...], wdown_ref[...])
    if final_norm:
        out = _rms(out, gfin_ref[...])
    o_ref[...] = out


def _ffn_layer(h, g, w_up, w_dw, b_dw, w_down, g_fin, *, final_norm, tm=512, cn=256):
    bsz, t, d = h.shape
    d_ff = w_down.shape[0]
    full = lambda a: pl.BlockSpec(a.shape, lambda b, i: (0,) * a.ndim)
    args = (g.reshape(1, d), w_up.astype(BF16), w_dw, b_dw.reshape(1, -1),
            w_down.astype(BF16), g_fin.reshape(1, d))
    return pl.pallas_call(
        functools.partial(_ffn_kernel, d_ff=d_ff, cn=cn, final_norm=final_norm),
        grid=(bsz, t // tm),
        in_specs=[pl.BlockSpec((None, tm, d), lambda b, i: (b, i, 0))] + [full(a) for a in args],
        out_specs=pl.BlockSpec((None, tm, d), lambda b, i: (b, i, 0)),
        out_shape=jax.ShapeDtypeStruct(h.shape, F32),
        scratch_shapes=[pltpu.VMEM((tm, d_ff), BF16),
                        pltpu.VMEM((2 * (d_ff // cn), V7X_SUBLANES, cn), F32)],
        compiler_params=pltpu.CompilerParams(
            dimension_semantics=("arbitrary", "arbitrary"),
            vmem_limit_bytes=V7X_VMEM_LIMIT_BYTES),
        name="ffn",
    )(h, *args)


def _conf_kernel(h_ref, g_ref, w1_ref, wdw_ref, bdw_ref, lng_ref, lnb_ref, w2_ref,
                 o_ref, zbuf_ref, ybuf_ref, *, rb, lb):
    tm, d = h_ref.shape
    width = wdw_ref.shape[0]
    first = CONF_HALO - (width - 1)

    @pl.when(pl.program_id(1) == 0)
    def _():
        zbuf_ref[0:CONF_HALO] = jnp.zeros((CONF_HALO, d), F32)

    @pl.when(pl.program_id(1) > 0)
    def _():
        zbuf_ref[0:CONF_HALO] = zbuf_ref[tm:tm + CONF_HALO]

    h = h_ref[...]
    hn = _rms(h, g_ref[...]).astype(BF16)
    a = _dot(hn, w1_ref[:, :d])
    gate = _dot(hn, w1_ref[:, d:])
    zbuf_ref[CONF_HALO:CONF_HALO + tm] = a * jax.nn.sigmoid(gate)

    for j in range(d // lb):
        lanes = slice(j * lb, (j + 1) * lb)
        wl = wdw_ref[:, lanes]
        bl = bdw_ref[:, lanes]

        def body(i, carry, lanes=lanes, wl=wl, bl=bl):
            r0 = pl.multiple_of(i * rb, rb)
            win = zbuf_ref[pl.ds(r0, rb + CONF_HALO), lanes]
            acc = jnp.broadcast_to(bl, (rb, lb))
            for k in range(width):
                acc = acc + win[first + k:first + k + rb] * wl[k:k + 1]
            ybuf_ref[pl.ds(r0, rb), lanes] = acc
            return carry

        lax.fori_loop(0, tm // rb, body, 0)

    y = ybuf_ref[...]
    mu = jnp.mean(y, axis=-1, keepdims=True)
    yc = y - mu
    yn = yc * lax.rsqrt(jnp.mean(yc * yc, axis=-1, keepdims=True) + EPS)
    yn = yn * lng_ref[...] + lnb_ref[...]
    o_ref[...] = h + _dot(jax.nn.silu(yn).astype(BF16), w2_ref[...])


def _conf_layer(h, g, w_pw1, w_dw, b_dw, ln_g, ln_b, w_pw2, *, tm=512, rb=32, lb=256):
    bsz, t, d = h.shape
    full = lambda a: pl.BlockSpec(a.shape, lambda b, i: (0,) * a.ndim)
    args = (g.reshape(1, d), w_pw1.astype(BF16), w_dw, b_dw.reshape(1, d),
            ln_g.reshape(1, d), ln_b.reshape(1, d), w_pw2.astype(BF16))
    return pl.pallas_call(
        functools.partial(_conf_kernel, rb=rb, lb=lb),
        grid=(bsz, t // tm),
        in_specs=[pl.BlockSpec((None, tm, d), lambda b, i: (b, i, 0))] + [full(a) for a in args],
        out_specs=pl.BlockSpec((None, tm, d), lambda b, i: (b, i, 0)),
        out_shape=jax.ShapeDtypeStruct(h.shape, F32),
        scratch_shapes=[pltpu.VMEM((tm + CONF_HALO, d), F32), pltpu.VMEM((tm, d), F32)],
        compiler_params=pltpu.CompilerParams(
            dimension_semantics=("arbitrary", "arbitrary"),
            vmem_limit_bytes=V7X_VMEM_LIMIT_BYTES),
        name="conformer",
    )(h, *args)


def _gla_kernel(h_ref, g_ref, win_ref, wgk_ref, bgk_ref, gn_ref, wo_ref,
                o_ref, proj_ref, lg_ref, oall_ref, st_ref, *, key_dim, val_dim):
    tm = h_ref.shape[0]
    dk = key_dim // GLA_HEADS
    dv = val_dim // GLA_HEADS
    k0, v0, g0, r0 = key_dim, 2 * key_dim, 2 * key_dim + val_dim, 2 * key_dim + 2 * val_dim

    @pl.when(pl.program_id(1) == 0)
    def _():
        st_ref[...] = jnp.zeros_like(st_ref)

    h = h_ref[...]
    hn = _rms(h, g_ref[...]).astype(BF16)
    proj_ref[...] = _dot(hn, win_ref[...])
    gk = _dot(proj_ref[:, r0:r0 + GLA_RANK_PAD].astype(BF16), wgk_ref[...]) + bgk_ref[...]
    lg_ref[...] = jax.nn.log_sigmoid(gk) * (1.0 / GLA_GATE_NORM)

    row = lax.broadcasted_iota(jnp.int32, (GLA_CHUNK, GLA_CHUNK), 0)
    col = lax.broadcasted_iota(jnp.int32, (GLA_CHUNK, GLA_CHUNK), 1)
    causal = row >= col
    tril = causal.astype(F32)

    for c in range(tm // GLA_CHUNK):
        r = slice(c * GLA_CHUNK, (c + 1) * GLA_CHUNK)
        b = jnp.dot(tril, lg_ref[r, :], precision=lax.Precision.HIGHEST,
                    preferred_element_type=F32)
        b_last = b[GLA_CHUNK - 1:GLA_CHUNK, :]
        k = proj_ref[r, k0:k0 + key_dim]
        qe = proj_ref[r, 0:key_dim] * (dk ** -0.5) * jnp.exp(b)
        ke = k * jnp.exp(-b)
        kd = k * jnp.exp(b_last - b)
        decay = jnp.exp(b_last)
        for hh in range(GLA_HEADS):
            ks = slice(hh * dk, (hh + 1) * dk)
            qeh = qe[:, ks].astype(BF16)
            v = proj_ref[r, v0 + hh * dv:v0 + (hh + 1) * dv]
            att = jnp.where(causal, _dot_nt(qeh, ke[:, ks].astype(BF16)), 0.0)
            st = st_ref[hh]
            o = _dot(att.astype(BF16), v.astype(BF16)) + _dot_nt(qeh, st.astype(BF16))
            st_ref[hh] = st * decay[:, ks] + _dot(v.T.astype(BF16), kd[:, ks].astype(BF16))
            o = o * lax.rsqrt(jnp.mean(o * o, axis=-1, keepdims=True) + EPS) * gn_ref[...]
            o = o * jax.nn.silu(proj_ref[r, g0 + hh * dv:g0 + (hh + 1) * dv])
            oall_ref[r, hh * dv:(hh + 1) * dv] = o.astype(BF16)

    o_ref[...] = h + _dot(oall_ref[...], wo_ref[...])


def _gla_layer(h, g, w_in, w_gk2, b_gk, g_norm, w_o, *, tm=256):
    bsz, t, d = h.shape
    rank, key_dim = w_gk2.shape
    val_dim = w_o.shape[0]
    n_main = 2 * key_dim + 2 * val_dim
    w_in_p = jnp.pad(w_in.astype(BF16), ((0, 0), (0, GLA_RANK_PAD - rank)))
    w_gk_p = jnp.pad(w_gk2.astype(BF16), ((0, GLA_RANK_PAD - rank), (0, 0)))
    full = lambda a: pl.BlockSpec(a.shape, lambda b, i: (0,) * a.ndim)
    args = (g.reshape(1, d), w_in_p, w_gk_p, b_gk.reshape(1, key_dim),
            g_norm.reshape(1, -1), w_o.astype(BF16))
    return pl.pallas_call(
        functools.partial(_gla_kernel, key_dim=key_dim, val_dim=val_dim),
        grid=(bsz, t // tm),
        in_specs=[pl.BlockSpec((None, tm, d), lambda b, i: (b, i, 0))] + [full(a) for a in args],
        out_specs=pl.BlockSpec((None, tm, d), lambda b, i: (b, i, 0)),
        out_shape=jax.ShapeDtypeStruct(h.shape, F32),
        scratch_shapes=[pltpu.VMEM((tm, n_main + GLA_RANK_PAD), F32),
                        pltpu.VMEM((tm, key_dim), F32),
                        pltpu.VMEM((tm, val_dim), BF16),
                        pltpu.VMEM((GLA_HEADS, val_dim // GLA_HEADS, key_dim // GLA_HEADS), F32)],
        compiler_params=pltpu.CompilerParams(
            dimension_semantics=("arbitrary", "arbitrary"),
            vmem_limit_bytes=V7X_VMEM_LIMIT_BYTES),
        name="gla",
    )(h, *args)


def kernel(x, gla_w_in, gla_w_gk2, gla_b_gk, gla_norm, gla_w_o, cm_w_pw1, cm_w_dw, cm_b_dw, cm_ln_g, cm_ln_b, cm_w_pw2, ffn_w_up, ffn_w_dw, ffn_b_dw, ffn_w_down, norm_mix, norm_ffn, norm_final):
    depth = norm_mix.shape[0]
    h = x
    for i in range(depth):
        j = i // 2
        if i % 2 == 0:
            h = _gla_layer(h, norm_mix[i], gla_w_in[j], gla_w_gk2[j], gla_b_gk[j],
                           gla_norm[j], gla_w_o[j])
        else:
            h = _conf_layer(h, norm_mix[i], cm_w_pw1[j], cm_w_dw[j], cm_b_dw[j],
                            cm_ln_g[j], cm_ln_b[j], cm_w_pw2[j])
        h = _ffn_layer(h, norm_ffn[i], ffn_w_up[i], ffn_w_dw[i], ffn_b_dw[i],
                       ffn_w_down[i], norm_final, final_norm=(i == depth - 1))
    return h
```

```python
import functools

import jax
import jax.numpy as jnp
from jax import lax
from jax.experimental import pallas as pl
from jax.experimental.pallas import tpu as pltpu

F32 = jnp.float32
BF16 = jnp.bfloat16
EPS = 1e-6

V7X_SUBLANES = 8
V7X_LANES = 128
V7X_VMEM_LIMIT_BYTES = 56 * 1024 * 1024

GLA_HEADS = 4
GLA_CHUNK = 64
GLA_GATE_NORM = 16.0
GLA_RANK_PAD = V7X_LANES

CONF_HALO = 32


def _rms(x, g):
    return x * lax.rsqrt(jnp.mean(x * x, axis=-1, keepdims=True) + EPS) * g


def _dot(a, b):
    return jnp.dot(a, b, preferred_element_type=F32)


def _dot_nt(a, b):
    return lax.dot_general(a, b, (((1,), (1,)), ((), ())), preferred_element_type=F32)


def _ffn_kernel(h_ref, g_ref, wup_ref, wdw_ref, bdw_ref, wdown_ref, gfin_ref,
                o_ref, act_ref, carry_ref, *, d_ff, cn, final_norm):
    tm = h_ref.shape[0]

    @pl.when(pl.program_id(1) == 0)
    def _():
        carry_ref[...] = jnp.zeros_like(carry_ref)

    h = h_ref[...]
    hn = _rms(h, g_ref[...]).astype(BF16)
    rows8 = lax.broadcasted_iota(jnp.int32, (V7X_SUBLANES, cn), 0)

    def conv(col0, idx):
        z = _dot(hn, wup_ref[:, col0:col0 + cn])
        prev = carry_ref[idx]
        carry_ref[idx] = z[tm - V7X_SUBLANES:tm]
        w = wdw_ref[:, col0:col0 + cn]
        y = z * w[2:3] + bdw_ref[:, col0:col0 + cn]
        for s in (1, 2):
            zs = pltpu.roll(z, s, 0)
            top = jnp.where(rows8 < s, pltpu.roll(prev, s, 0), zs[:V7X_SUBLANES])
            zs = jnp.concatenate([top, zs[V7X_SUBLANES:]], axis=0)
            y = y + zs * w[2 - s:3 - s]
        return y

    for c in range(d_ff // cn):
        val = conv(c * cn, 2 * c)
        gate = conv(d_ff + c * cn, 2 * c + 1)
        act_ref[:, c * cn:(c + 1) * cn] = (jax.nn.silu(gate) * val).astype(BF16)

    out = h + _dot(act_ref[...], wdown_ref[...])
    if final_norm:
        out = _rms(out, gfin_ref[...])
    o_ref[...] = out


def _ffn_layer(h, g, w_up, w_dw, b_dw, w_down, g_fin, *, final_norm, tm=512, cn=256):
    bsz, t, d = h.shape
    d_ff = w_down.shape[0]
    full = lambda a: pl.BlockSpec(a.shape, lambda b, i: (0,) * a.ndim)
    args = (g.reshape(1, d), w_up.astype(BF16), w_dw, b_dw.reshape(1, -1),
            w_down.astype(BF16), g_fin.reshape(1, d))
    return pl.pallas_call(
        functools.partial(_ffn_kernel, d_ff=d_ff, cn=cn, final_norm=final_norm),
        grid=(bsz, t // tm),
        in_specs=[pl.BlockSpec((None, tm, d), lambda b, i: (b, i, 0))] + [full(a) for a in args],
        out_specs=pl.BlockSpec((None, tm, d), lambda b, i: (b, i, 0)),
        out_shape=jax.ShapeDtypeStruct(h.shape, F32),
        scratch_shapes=[pltpu.VMEM((tm, d_ff), BF16),
                        pltpu.VMEM((2 * (d_ff // cn), V7X_SUBLANES, cn), F32)],
        compiler_params=pltpu.CompilerParams(
            dimension_semantics=("arbitrary", "arbitrary"),
            vmem_limit_bytes=V7X_VMEM_LIMIT_BYTES),
        name="ffn",
    )(h, *args)


def _conf_kernel(h_ref, g_ref, w1_ref, wdw_ref, bdw_ref, lng_ref, lnb_ref, w2_ref,
                 o_ref, zt_ref, yt_ref, *, pz, py, ob):
    tm, d = h_ref.shape
    nslab = d // V7X_LANES
    width = wdw_ref.shape[0]
    first = CONF_HALO - (width - 1)

    @pl.when(pl.program_id(1) == 0)
    def _():
        for j in range(nslab):
            zt_ref[j * pz:j * pz + CONF_HALO] = jnp.zeros((CONF_HALO, V7X_LANES), F32)

    @pl.when(pl.program_id(1) > 0)
    def _():
        for j in range(nslab):
            zt_ref[j * pz:j * pz + CONF_HALO] = zt_ref[j * pz + tm:j * pz + tm + CONF_HALO]

    h = h_ref[...]
    hn = _rms(h, g_ref[...]).astype(BF16)
    a = _dot(hn, w1_ref[:, :d])
    gate = _dot(hn, w1_ref[:, d:])
    z = a * jax.nn.sigmoid(gate)
    for j in range(nslab):
        zt_ref[j * pz + CONF_HALO:j * pz + CONF_HALO + tm] = z[:, j * V7X_LANES:(j + 1) * V7X_LANES]

    w = [wdw_ref[k] for k in range(width)]
    bias = bdw_ref[...]

    def body(i, carry):
        t0 = pl.multiple_of(i * ob, ob)
        acc = [bias] * ob
        for u in range(ob + width - 1):
            zu = zt_ref[pl.ds(t0 + u + first, V7X_SUBLANES, stride=pz), :]
            for o in range(ob):
                if 0 <= u - o < width:
                    acc[o] = acc[o] + zu * w[u - o]
        for o in range(ob):
            yt_ref[pl.ds(t0 + o, V7X_SUBLANES, stride=py), :] = acc[o]
        return carry

    lax.fori_loop(0, tm // ob, body, 0)

    y = jnp.concatenate([yt_ref[j * py:j * py + tm] for j in range(nslab)], axis=1)
    mu = jnp.mean(y, axis=-1, keepdims=True)
    yc = y - mu
    yn = yc * lax.rsqrt(jnp.mean(yc * yc, axis=-1, keepdims=True) + EPS)
    yn = yn * lng_ref[...] + lnb_ref[...]
    o_ref[...] = h + _dot(jax.nn.silu(yn).astype(BF16), w2_ref[...])


def _slab_pitch(rows):
    p = -(-rows // V7X_SUBLANES)
    return V7X_SUBLANES * (p + 1 - p % 2)


def _conf_layer(h, g, w_pw1, w_dw, b_dw, ln_g, ln_b, w_pw2, *, tm=512, ob=8):
    bsz, t, d = h.shape
    nslab = d // V7X_LANES
    pz, py = _slab_pitch(tm + CONF_HALO), _slab_pitch(tm)
    full = lambda a: pl.BlockSpec(a.shape, lambda b, i: (0,) * a.ndim)
    args = (g.reshape(1, d), w_pw1.astype(BF16), w_dw.reshape(-1, nslab, V7X_LANES),
            b_dw.reshape(nslab, V7X_LANES),
            ln_g.reshape(1, d), ln_b.reshape(1, d), w_pw2.astype(BF16))
    return pl.pallas_call(
        functools.partial(_conf_kernel, pz=pz, py=py, ob=ob),
        grid=(bsz, t // tm),
        in_specs=[pl.BlockSpec((None, tm, d), lambda b, i: (b, i, 0))] + [full(a) for a in args],
        out_specs=pl.BlockSpec((None, tm, d), lambda b, i: (b, i, 0)),
        out_shape=jax.ShapeDtypeStruct(h.shape, F32),
        scratch_shapes=[pltpu.VMEM((nslab * pz, V7X_LANES), F32),
                        pltpu.VMEM((nslab * py, V7X_LANES), F32)],
        compiler_params=pltpu.CompilerParams(
            dimension_semantics=("arbitrary", "arbitrary"),
            vmem_limit_bytes=V7X_VMEM_LIMIT_BYTES),
        name="conformer",
    )(h, *args)


def _gla_kernel(h_ref, g_ref, win_ref, wgk_ref, bgk_ref, gn_ref, wo_ref,
                o_ref, proj_ref, lg_ref, oall_ref, st_ref, *, key_dim, val_dim):
    tm = h_ref.shape[0]
    dk = key_dim // GLA_HEADS
    dv = val_dim // GLA_HEADS
    k0, v0, g0, r0 = key_dim, 2 * key_dim, 2 * key_dim + val_dim, 2 * key_dim + 2 * val_dim

    @pl.when(pl.program_id(1) == 0)
    def _():
        st_ref[...] = jnp.zeros_like(st_ref)

    h = h_ref[...]
    hn = _rms(h, g_ref[...]).astype(BF16)
    proj_ref[...] = _dot(hn, win_ref[...])
    gk = _dot(proj_ref[:, r0:r0 + GLA_RANK_PAD].astype(BF16), wgk_ref[...]) + bgk_ref[...]
    lg_ref[...] = jax.nn.log_sigmoid(gk) * (1.0 / GLA_GATE_NORM)

    row = lax.broadcasted_iota(jnp.int32, (GLA_CHUNK, GLA_CHUNK), 0)
    col = lax.broadcasted_iota(jnp.int32, (GLA_CHUNK, GLA_CHUNK), 1)
    causal = row >= col
    tril = causal.astype(F32)

    for c in range(tm // GLA_CHUNK):
        r = slice(c * GLA_CHUNK, (c + 1) * GLA_CHUNK)
        b = jnp.dot(tril, lg_ref[r, :], precision=lax.Precision.HIGHEST,
                    preferred_element_type=F32)
        b_last = b[GLA_CHUNK - 1:GLA_CHUNK, :]
        k = proj_ref[r, k0:k0 + key_dim]
        qe = proj_ref[r, 0:key_dim] * (dk ** -0.5) * jnp.exp(b)
        ke = k * jnp.exp(-b)
        kd = k * jnp.exp(b_last - b)
        decay = jnp.exp(b_last)
        for hh in range(GLA_HEADS):
            ks = slice(hh * dk, (hh + 1) * dk)
            qeh = qe[:, ks].astype(BF16)
            v = proj_ref[r, v0 + hh * dv:v0 + (hh + 1) * dv]
            att = jnp.where(causal, _dot_nt(qeh, ke[:, ks].astype(BF16)), 0.0)
            st = st_ref[hh]
            o = _dot(att.astype(BF16), v.astype(BF16)) + _dot_nt(qeh, st.astype(BF16))
            st_ref[hh] = st * decay[:, ks] + _dot(v.T.astype(BF16), kd[:, ks].astype(BF16))
            o = o * lax.rsqrt(jnp.mean(o * o, axis=-1, keepdims=True) + EPS) * gn_ref[...]
            o = o * jax.nn.silu(proj_ref[r, g0 + hh * dv:g0 + (hh + 1) * dv])
            oall_ref[r, hh * dv:(hh + 1) * dv] = o.astype(BF16)

    o_ref[...] = h + _dot(oall_ref[...], wo_ref[...])


def _gla_layer(h, g, w_in, w_gk2, b_gk, g_norm, w_o, *, tm=256):
    bsz, t, d = h.shape
    rank, key_dim = w_gk2.shape
    val_dim = w_o.shape[0]
    n_main = 2 * key_dim + 2 * val_dim
    w_in_p = jnp.pad(w_in.astype(BF16), ((0, 0), (0, GLA_RANK_PAD - rank)))
    w_gk_p = jnp.pad(w_gk2.astype(BF16), ((0, GLA_RANK_PAD - rank), (0, 0)))
    full = lambda a: pl.BlockSpec(a.shape, lambda b, i: (0,) * a.ndim)
    args = (g.reshape(1, d), w_in_p, w_gk_p, b_gk.reshape(1, key_dim),
            g_norm.reshape(1, -1), w_o.astype(BF16))
    return pl.pallas_call(
        functools.partial(_gla_kernel, key_dim=key_dim, val_dim=val_dim),
        grid=(bsz, t // tm),
        in_specs=[pl.BlockSpec((None, tm, d), lambda b, i: (b, i, 0))] + [full(a) for a in args],
        out_specs=pl.BlockSpec((None, tm, d), lambda b, i: (b, i, 0)),
        out_shape=jax.ShapeDtypeStruct(h.shape, F32),
        scratch_shapes=[pltpu.VMEM((tm, n_main + GLA_RANK_PAD), F32),
                        pltpu.VMEM((tm, key_dim), F32),
                        pltpu.VMEM((tm, val_dim), BF16),
                        pltpu.VMEM((GLA_HEADS, val_dim // GLA_HEADS, key_dim // GLA_HEADS), F32)],
        compiler_params=pltpu.CompilerParams(
            dimension_semantics=("arbitrary", "arbitrary"),
            vmem_limit_bytes=V7X_VMEM_LIMIT_BYTES),
        name="gla",
    )(h, *args)


def kernel(x, gla_w_in, gla_w_gk2, gla_b_gk, gla_norm, gla_w_o, cm_w_pw1, cm_w_dw, cm_b_dw, cm_ln_g, cm_ln_b, cm_w_pw2, ffn_w_up, ffn_w_dw, ffn_b_dw, ffn_w_down, norm_mix, norm_ffn, norm_final):
    depth = norm_mix.shape[0]
    h = x
    for i in range(depth):
        j = i // 2
        if i % 2 == 0:
            h = _gla_layer(h, norm_mix[i], gla_w_in[j], gla_w_gk2[j], gla_b_gk[j],
                           gla_norm[j], gla_w_o[j])
        else:
            h = _conf_layer(h, norm_mix[i], cm_w_pw1[j], cm_w_dw[j], cm_b_dw[j],
                            cm_ln_g[j], cm_ln_b[j], cm_w_pw2[j])
        h = _ffn_layer(h, norm_ffn[i], ffn_w_up[i], ffn_w_dw[i], ffn_b_dw[i],
                       ffn_w_down[i], norm_final, final_norm=(i == depth - 1))
    return h
```

```python
import functools

import jax
import jax.numpy as jnp
from jax import lax
from jax.experimental import pallas as pl
from jax.experimental.pallas import tpu as pltpu

F32 = jnp.float32
BF16 = jnp.bfloat16
EPS = 1e-6

V7X_SUBLANES = 8
V7X_LANES = 128
V7X_VMEM_LIMIT_BYTES = 56 * 1024 * 1024

GLA_HEADS = 4
GLA_CHUNK = 64
GLA_GATE_NORM = 16.0
GLA_RANK_PAD = V7X_LANES

CONF_HALO = 32


def _rms(x, g):
    return x * lax.rsqrt(jnp.mean(x * x, axis=-1, keepdims=True) + EPS) * g


def _dot(a, b):
    return jnp.dot(a, b, preferred_element_type=F32)


def _dot_nt(a, b):
    return lax.dot_general(a, b, (((1,), (1,)), ((), ())), preferred_element_type=F32)


def _pair_specs(bsz, t, d, tm):
    nt = t // tm
    last = bsz * nt - 1

    def nxt(b, k):
        item = jnp.minimum(b * nt + 2 * k + 2, last)
        return (item // nt, item % nt, 0)

    cur = pl.BlockSpec((None, 2 * tm, d), lambda b, k: (b, k, 0))
    return cur, pl.BlockSpec((None, tm, d), nxt)


def _ffn_kernel(h_ref, g_ref, wup_ref, wdw_ref, bdw_ref, wdown_ref, gfin_ref,
                o_ref, act_ref, carry_ref, *, d_ff, cn, final_norm):
    tm = h_ref.shape[0]

    @pl.when(pl.program_id(1) == 0)
    def _():
        carry_ref[...] = jnp.zeros_like(carry_ref)

    h = h_ref[...]
    hn = _rms(h, g_ref[...]).astype(BF16)
    rows8 = lax.broadcasted_iota(jnp.int32, (V7X_SUBLANES, cn), 0)

    def conv(col0, idx):
        z = _dot(hn, wup_ref[:, col0:col0 + cn])
        prev = carry_ref[idx]
        carry_ref[idx] = z[tm - V7X_SUBLANES:tm]
        w = wdw_ref[:, col0:col0 + cn]
        y = z * w[2:3] + bdw_ref[:, col0:col0 + cn]
        for s in (1, 2):
            zs = pltpu.roll(z, s, 0)
            top = jnp.where(rows8 < s, pltpu.roll(prev, s, 0), zs[:V7X_SUBLANES])
            zs = jnp.concatenate([top, zs[V7X_SUBLANES:]], axis=0)
            y = y + zs * w[2 - s:3 - s]
        return y

    for c in range(d_ff // cn):
        val = conv(c * cn, 2 * c)
        gate = conv(d_ff + c * cn, 2 * c + 1)
        act_ref[:, c * cn:(c + 1) * cn] = (jax.nn.silu(gate) * val).astype(BF16)

    out = h + _dot(act_ref[...], wdown_ref[...])
    if final_norm:
        out = _rms(out, gfin_ref[...])
    o_ref[...] = out


def _ffn_layer(h, g, w_up, w_dw, b_dw, w_down, g_fin, *, final_norm, tm=512, cn=256):
    bsz, t, d = h.shape
    d_ff = w_down.shape[0]
    full = lambda a: pl.BlockSpec(a.shape, lambda b, i: (0,) * a.ndim)
    args = (g.reshape(1, d), w_up.astype(BF16), w_dw, b_dw.reshape(1, -1),
            w_down.astype(BF16), g_fin.reshape(1, d))
    return pl.pallas_call(
        functools.partial(_ffn_kernel, d_ff=d_ff, cn=cn, final_norm=final_norm),
        grid=(bsz, t // tm),
        in_specs=[pl.BlockSpec((None, tm, d), lambda b, i: (b, i, 0))] + [full(a) for a in args],
        out_specs=pl.BlockSpec((None, tm, d), lambda b, i: (b, i, 0)),
        out_shape=jax.ShapeDtypeStruct(h.shape, F32),
        scratch_shapes=[pltpu.VMEM((tm, d_ff), BF16),
                        pltpu.VMEM((2 * (d_ff // cn), V7X_SUBLANES, cn), F32)],
        compiler_params=pltpu.CompilerParams(
            dimension_semantics=("arbitrary", "arbitrary"),
            vmem_limit_bytes=V7X_VMEM_LIMIT_BYTES),
        name="ffn",
    )(h, *args)


def _conf_project_steps(read_h, g_ref, w1_ref, hn_ref, z_ref, *, pz, col_block, row_block):
    tm, d = hn_ref.shape

    def norm(r):
        def step():
            hn_ref[r, :] = _rms(read_h(r), g_ref[...]).astype(BF16)
        return step

    def glu(r, c0):
        def step():
            hn = hn_ref[r, :]
            z = (_dot(hn, w1_ref[:, c0:c0 + col_block])
                 * jax.nn.sigmoid(_dot(hn, w1_ref[:, d + c0:d + c0 + col_block])))
            for j in range(col_block // V7X_LANES):
                base = (c0 // V7X_LANES + j) * pz + CONF_HALO
                z_ref[base + r.start:base + r.stop] = z[:, j * V7X_LANES:(j + 1) * V7X_LANES]
        return step

    rows = [slice(r, r + row_block) for r in range(0, tm, row_block)]
    return [norm(r) for r in rows] + [glu(r, c) for c in range(0, d, col_block) for r in rows]


def _conf_mix(h, z_ref, wdw_ref, bdw_ref, lng_ref, lnb_ref, w2_ref, yt_ref, fillers,
              *, pz, py, ob):
    tm, d = h.shape
    nslab = d // V7X_LANES
    width = wdw_ref.shape[0]
    first = CONF_HALO - (width - 1)
    n_blocks = tm // ob
    done = 0

    for i in range(n_blocks):
        while done * n_blocks < (i + 1) * len(fillers):
            fillers[done]()
            done += 1
        t0 = i * ob
        acc = [bdw_ref[...]] * ob
        for u in range(ob + width - 1):
            zu = z_ref[pl.ds(t0 + u + first, V7X_SUBLANES, stride=pz), :]
            for o in range(ob):
                if 0 <= u - o < width:
                    acc[o] = acc[o] + zu * wdw_ref[u - o]
        for o in range(ob):
            yt_ref[pl.ds(t0 + o, V7X_SUBLANES, stride=py), :] = acc[o]

    assert done == len(fillers)
    y = jnp.concatenate([yt_ref[j * py:j * py + tm] for j in range(nslab)], axis=1)
    mu = jnp.mean(y, axis=-1, keepdims=True)
    yc = y - mu
    yn = yc * lax.rsqrt(jnp.mean(yc * yc, axis=-1, keepdims=True) + EPS)
    yn = yn * lng_ref[...] + lnb_ref[...]
    return h + _dot(jax.nn.silu(yn).astype(BF16), w2_ref[...])


def _conf_kernel(hc_ref, hx_ref, g_ref, w1_ref, wdw_ref, bdw_ref, lng_ref, lnb_ref, w2_ref,
                 o_ref, hn_ref, z0_ref, z1_ref, yt_ref, *, pz, py, ob):
    tm, d = hx_ref.shape
    nslab = d // V7X_LANES
    steps = functools.partial(_conf_project_steps, g_ref=g_ref, w1_ref=w1_ref, hn_ref=hn_ref,
                              pz=pz, col_block=2 * V7X_LANES, row_block=tm // 2)
    mix = functools.partial(_conf_mix, wdw_ref=wdw_ref, bdw_ref=bdw_ref, lng_ref=lng_ref,
                            lnb_ref=lnb_ref, w2_ref=w2_ref, yt_ref=yt_ref, pz=pz, py=py, ob=ob)

    def set_halo(dst_ref, src_ref):
        for j in range(nslab):
            rows = slice(j * pz, j * pz + CONF_HALO)
            if src_ref is None:
                dst_ref[rows] = jnp.zeros((CONF_HALO, V7X_LANES), F32)
            else:
                dst_ref[rows] = src_ref[j * pz + tm:j * pz + tm + CONF_HALO]

    @pl.when((pl.program_id(0) == 0) & (pl.program_id(1) == 0))
    def _():
        for step in steps(lambda r: hc_ref[r, :], z_ref=z0_ref):
            step()

    @pl.when(pl.program_id(1) == 0)
    def _():
        set_halo(z0_ref, None)

    @pl.when(pl.program_id(1) > 0)
    def _():
        set_halo(z0_ref, z1_ref)

    o_ref[0:tm] = mix(hc_ref[0:tm], z0_ref,
                      fillers=steps(lambda r: hc_ref[tm + r.start:tm + r.stop, :], z_ref=z1_ref))
    set_halo(z1_ref, z0_ref)
    o_ref[tm:2 * tm] = mix(hc_ref[tm:2 * tm], z1_ref,
                           fillers=steps(lambda r: hx_ref[r, :], z_ref=z0_ref))


def _slab_pitch(rows):
    p = -(-rows // V7X_SUBLANES)
    return V7X_SUBLANES * (p + 1 - p % 2)


def _conf_layer(h, g, w_pw1, w_dw, b_dw, ln_g, ln_b, w_pw2, *, tm=512, ob=8):
    bsz, t, d = h.shape
    nslab = d // V7X_LANES
    pz, py = _slab_pitch(tm + CONF_HALO), _slab_pitch(tm)
    full = lambda a: pl.BlockSpec(a.shape, lambda b, k: (0,) * a.ndim)
    args = (g.reshape(1, d), w_pw1.astype(BF16), w_dw.reshape(-1, nslab, V7X_LANES),
            b_dw.reshape(nslab, V7X_LANES),
            ln_g.reshape(1, d), ln_b.reshape(1, d), w_pw2.astype(BF16))
    cur, nxt = _pair_specs(bsz, t, d, tm)
    zbuf = pltpu.VMEM((nslab * pz, V7X_LANES), F32)
    return pl.pallas_call(
        functools.partial(_conf_kernel, pz=pz, py=py, ob=ob),
        grid=(bsz, t // (2 * tm)),
        in_specs=[cur, nxt] + [full(a) for a in args],
        out_specs=cur,
        out_shape=jax.ShapeDtypeStruct(h.shape, F32),
        scratch_shapes=[pltpu.VMEM((tm, d), BF16), zbuf, zbuf,
                        pltpu.VMEM((nslab * py, V7X_LANES), F32)],
        compiler_params=pltpu.CompilerParams(
            dimension_semantics=("arbitrary", "arbitrary"),
            vmem_limit_bytes=V7X_VMEM_LIMIT_BYTES),
        name="conformer",
    )(h, h, *args)


def _gla_project_steps(read_h, g_ref, win_ref, wgk_ref, bgk_ref, hn_ref, proj_ref, lg_ref,
                       *, r0, col_block, row_block):
    tm = hn_ref.shape[0]
    n_total = win_ref.shape[1]

    def norm():
        hn_ref[...] = _rms(read_h(), g_ref[...]).astype(BF16)

    def project(c0, c1):
        def step():
            proj_ref[:, c0:c1] = _dot(hn_ref[...], win_ref[:, c0:c1])
        return step

    def gate(r):
        def step():
            gk = _dot(proj_ref[r, r0:r0 + GLA_RANK_PAD].astype(BF16), wgk_ref[...]) + bgk_ref[...]
            lg_ref[r, :] = jax.nn.log_sigmoid(gk) * (1.0 / GLA_GATE_NORM)
        return step

    return ([norm]
            + [project(c, min(c + col_block, n_total)) for c in range(0, n_total, col_block)]
            + [gate(slice(r, r + row_block)) for r in range(0, tm, row_block)])


def _gla_mix(h, proj_ref, lg_ref, gn_ref, wo_ref, st_ref, oall_ref, fillers, *, key_dim, val_dim):
    tm = h.shape[0]
    dk = key_dim // GLA_HEADS
    dv = val_dim // GLA_HEADS
    k0, v0, g0 = key_dim, 2 * key_dim, 2 * key_dim + val_dim
    row = lax.broadcasted_iota(jnp.int32, (GLA_CHUNK, GLA_CHUNK), 0)
    col = lax.broadcasted_iota(jnp.int32, (GLA_CHUNK, GLA_CHUNK), 1)
    causal = row >= col
    tril = causal.astype(BF16)
    n_units = (tm // GLA_CHUNK) * GLA_HEADS
    done = 0

    for c in range(tm // GLA_CHUNK):
        r = slice(c * GLA_CHUNK, (c + 1) * GLA_CHUNK)
        lg = lg_ref[r, :]
        lg_hi = lg.astype(BF16)
        rem = lg - lg_hi.astype(F32)
        lg_mid = rem.astype(BF16)
        lg_lo = (rem - lg_mid.astype(F32)).astype(BF16)
        b = _dot(tril, lg_hi) + _dot(tril, lg_mid) + _dot(tril, lg_lo)
        b_last = b[GLA_CHUNK - 1:GLA_CHUNK, :]
        k = proj_ref[r, k0:k0 + key_dim]
        qe = proj_ref[r, 0:key_dim] * (dk ** -0.5) * jnp.exp(b)
        ke = k * jnp.exp(-b)
        kd = k * jnp.exp(b_last - b)
        decay = jnp.exp(b_last)
        for hh in range(GLA_HEADS):
            unit = c * GLA_HEADS + hh
            while done * n_units < (unit + 1) * len(fillers):
                fillers[done]()
                done += 1
            ks = slice(hh * dk, (hh + 1) * dk)
            qeh = qe[:, ks].astype(BF16)
            v = proj_ref[r, v0 + hh * dv:v0 + (hh + 1) * dv]
            att = jnp.where(causal, _dot_nt(qeh, ke[:, ks].astype(BF16)), 0.0)
            st = st_ref[hh]
            o = _dot(att.astype(BF16), v.astype(BF16)) + _dot_nt(qeh, st.astype(BF16))
            st_ref[hh] = st * decay[:, ks] + _dot(v.T.astype(BF16), kd[:, ks].astype(BF16))
            o = o * lax.rsqrt(jnp.mean(o * o, axis=-1, keepdims=True) + EPS) * gn_ref[...]
            o = o * jax.nn.silu(proj_ref[r, g0 + hh * dv:g0 + (hh + 1) * dv])
            oall_ref[r, hh * dv:(hh + 1) * dv] = o.astype(BF16)

    assert done == len(fillers)
    return h + _dot(oall_ref[...], wo_ref[...])


def _gla_kernel(hc_ref, hx_ref, g_ref, win_ref, wgk_ref, bgk_ref, gn_ref, wo_ref, o_ref,
                hn_ref, proj0, proj1, lg0, lg1, oall0, oall1, st_ref, *, key_dim, val_dim):
    tm = hx_ref.shape[0]
    steps = functools.partial(_gla_project_steps, g_ref=g_ref, win_ref=win_ref, wgk_ref=wgk_ref,
                              bgk_ref=bgk_ref, hn_ref=hn_ref, r0=2 * key_dim + 2 * val_dim,
                              col_block=2 * V7X_LANES, row_block=GLA_CHUNK)
    mix = functools.partial(_gla_mix, gn_ref=gn_ref, wo_ref=wo_ref, st_ref=st_ref,
                            key_dim=key_dim, val_dim=val_dim)

    @pl.when((pl.program_id(0) == 0) & (pl.program_id(1) == 0))
    def _():
        for step in steps(lambda: hc_ref[0:tm], proj_ref=proj0, lg_ref=lg0):
            step()

    @pl.when(pl.program_id(1) == 0)
    def _():
        st_ref[...] = jnp.zeros_like(st_ref)

    o_ref[0:tm] = mix(hc_ref[0:tm], proj0, lg0, oall_ref=oall0,
                      fillers=steps(lambda: hc_ref[tm:2 * tm], proj_ref=proj1, lg_ref=lg1))
    o_ref[tm:2 * tm] = mix(hc_ref[tm:2 * tm], proj1, lg1, oall_ref=oall1,
                           fillers=steps(lambda: hx_ref[...], proj_ref=proj0, lg_ref=lg0))


def _gla_layer(h, g, w_in, w_gk2, b_gk, g_norm, w_o, *, tm=256):
    bsz, t, d = h.shape
    rank, key_dim = w_gk2.shape
    val_dim = w_o.shape[0]
    n_main = 2 * key_dim + 2 * val_dim
    w_in_p = jnp.pad(w_in.astype(BF16), ((0, 0), (0, GLA_RANK_PAD - rank)))
    w_gk_p = jnp.pad(w_gk2.astype(BF16), ((0, GLA_RANK_PAD - rank), (0, 0)))
    full = lambda a: pl.BlockSpec(a.shape, lambda b, k: (0,) * a.ndim)
    args = (g.reshape(1, d), w_in_p, w_gk_p, b_gk.reshape(1, key_dim),
            g_norm.reshape(1, -1), w_o.astype(BF16))
    cur, nxt = _pair_specs(bsz, t, d, tm)
    proj = pltpu.VMEM((tm, n_main + GLA_RANK_PAD), F32)
    lg = pltpu.VMEM((tm, key_dim), F32)
    oall = pltpu.VMEM((tm, val_dim), BF16)
    return pl.pallas_call(
        functools.partial(_gla_kernel, key_dim=key_dim, val_dim=val_dim),
        grid=(bsz, t // (2 * tm)),
        in_specs=[cur, nxt] + [full(a) for a in args],
        out_specs=cur,
        out_shape=jax.ShapeDtypeStruct(h.shape, F32),
        scratch_shapes=[pltpu.VMEM((tm, d), BF16), proj, proj, lg, lg, oall, oall,
                        pltpu.VMEM((GLA_HEADS, val_dim // GLA_HEADS, key_dim // GLA_HEADS), F32)],
        compiler_params=pltpu.CompilerParams(
            dimension_semantics=("arbitrary", "arbitrary"),
            vmem_limit_bytes=V7X_VMEM_LIMIT_BYTES),
        name="gla",
    )(h, h, *args)


def kernel(x, gla_w_in, gla_w_gk2, gla_b_gk, gla_norm, gla_w_o, cm_w_pw1, cm_w_dw, cm_b_dw, cm_ln_g, cm_ln_b, cm_w_pw2, ffn_w_up, ffn_w_dw, ffn_b_dw, ffn_w_down, norm_mix, norm_ffn, norm_final):
    depth = norm_mix.shape[0]
    h = x
    for i in range(depth):
        j = i // 2
        if i % 2 == 0:
            h = _gla_layer(h, norm_mix[i], gla_w_in[j], gla_w_gk2[j], gla_b_gk[j],
                           gla_norm[j], gla_w_o[j])
        else:
            h = _conf_layer(h, norm_mix[i], cm_w_pw1[j], cm_w_dw[j], cm_b_dw[j],
                            cm_ln_g[j], cm_ln_b[j], cm_w_pw2[j])
        h = _ffn_layer(h, norm_ffn[i], ffn_w_up[i], ffn_w_dw[i], ffn_b_dw[i],
                       ffn_w_down[i], norm_final, final_norm=(i == depth - 1))
    return h
```

```python
import functools

import jax
import jax.numpy as jnp
from jax import lax
from jax.experimental import pallas as pl
from jax.experimental.pallas import tpu as pltpu

F32 = jnp.float32
BF16 = jnp.bfloat16
EPS = 1e-6

V7X_SUBLANES = 8
V7X_LANES = 128
V7X_VMEM_LIMIT_BYTES = 56 * 1024 * 1024

GLA_HEADS = 4
GLA_CHUNK = 64
GLA_GATE_NORM = 16.0
GLA_RANK_PAD = V7X_LANES

CONF_HALO = 32


def _rms(x, g):
    return x * lax.rsqrt(jnp.mean(x * x, axis=-1, keepdims=True) + EPS) * g


def _dot(a, b):
    return jnp.dot(a, b, preferred_element_type=F32)


def _dot_nt(a, b):
    return lax.dot_general(a, b, (((1,), (1,)), ((), ())), preferred_element_type=F32)


def _full_spec(a):
    return pl.BlockSpec(a.shape, lambda b, k: (0,) * a.ndim)


def _layer_spec(a, layer):
    return pl.BlockSpec((None,) + a.shape[1:], lambda b, k: (layer,) + (0,) * (a.ndim - 1))


def _pair_specs(bsz, t, d, tm):
    nt = t // tm
    last = bsz * nt - 1

    def nxt(b, k):
        item = jnp.minimum(b * nt + 2 * k + 2, last)
        return (item // nt, item % nt, 0)

    cur = pl.BlockSpec((None, 2 * tm, d), lambda b, k: (b, k, 0))
    return cur, pl.BlockSpec((None, tm, d), nxt)


def _ffn_kernel(h_ref, g_ref, wup_ref, wdw_ref, bdw_ref, wdown_ref, gfin_ref,
                o_ref, act_ref, carry_ref, *, d_ff, cn, final_norm):
    tm = h_ref.shape[0]

    @pl.when(pl.program_id(1) == 0)
    def _():
        carry_ref[...] = jnp.zeros_like(carry_ref)

    h = h_ref[...]
    hn = _rms(h, g_ref[...]).astype(BF16)
    rows8 = lax.broadcasted_iota(jnp.int32, (V7X_SUBLANES, cn), 0)

    def conv(col0, idx):
        z = _dot(hn, wup_ref[:, col0:col0 + cn])
        prev = carry_ref[idx]
        carry_ref[idx] = z[tm - V7X_SUBLANES:tm]
        w = wdw_ref[:, col0:col0 + cn]
        y = z * w[2:3] + bdw_ref[:, col0:col0 + cn]
        for s in (1, 2):
            zs = pltpu.roll(z, s, 0)
            top = jnp.where(rows8 < s, pltpu.roll(prev, s, 0), zs[:V7X_SUBLANES])
            zs = jnp.concatenate([top, zs[V7X_SUBLANES:]], axis=0)
            y = y + zs * w[2 - s:3 - s]
        return y

    for c in range(d_ff // cn):
        val = conv(c * cn, 2 * c)
        gate = conv(d_ff + c * cn, 2 * c + 1)
        act_ref[:, c * cn:(c + 1) * cn] = (jax.nn.silu(gate) * val).astype(BF16)

    out = h + _dot(act_ref[...], wdown_ref[...])
    if final_norm:
        out = _rms(out, gfin_ref[...])
    o_ref[...] = out


def _ffn_layer(h, g, w_up_all, w_dw, b_dw, w_down_all, g_fin, *, layer, final_norm, tm=512, cn=256):
    bsz, t, d = h.shape
    d_ff = w_down_all.shape[1]
    args = (g.reshape(1, d), w_up_all, w_dw, b_dw.reshape(1, -1), w_down_all, g_fin.reshape(1, d))
    specs = [_layer_spec(a, layer) if a.ndim == 3 else _full_spec(a) for a in args]
    return pl.pallas_call(
        functools.partial(_ffn_kernel, d_ff=d_ff, cn=cn, final_norm=final_norm),
        grid=(bsz, t // tm),
        in_specs=[pl.BlockSpec((None, tm, d), lambda b, i: (b, i, 0))] + specs,
        out_specs=pl.BlockSpec((None, tm, d), lambda b, i: (b, i, 0)),
        out_shape=jax.ShapeDtypeStruct(h.shape, F32),
        scratch_shapes=[pltpu.VMEM((tm, d_ff), BF16),
                        pltpu.VMEM((2 * (d_ff // cn), V7X_SUBLANES, cn), F32)],
        compiler_params=pltpu.CompilerParams(
            dimension_semantics=("arbitrary", "arbitrary"),
            vmem_limit_bytes=V7X_VMEM_LIMIT_BYTES),
        name="ffn",
    )(h, *args)


def _conf_project_steps(read_h, g_ref, w1_ref, hn_ref, z_ref, *, pz, col_block, row_block):
    tm, d = hn_ref.shape

    def norm(r):
        def step():
            hn_ref[r, :] = _rms(read_h(r), g_ref[...]).astype(BF16)
        return step

    def glu(r, c0):
        def step():
            hn = hn_ref[r, :]
            z = (_dot(hn, w1_ref[:, c0:c0 + col_block])
                 * jax.nn.sigmoid(_dot(hn, w1_ref[:, d + c0:d + c0 + col_block])))
            for j in range(col_block // V7X_LANES):
                base = (c0 // V7X_LANES + j) * pz + CONF_HALO
                z_ref[base + r.start:base + r.stop] = z[:, j * V7X_LANES:(j + 1) * V7X_LANES]
        return step

    rows = [slice(r, r + row_block) for r in range(0, tm, row_block)]
    return [norm(r) for r in rows] + [glu(r, c) for c in range(0, d, col_block) for r in rows]


def _conf_mix(h, z_ref, wdw_ref, bdw_ref, lng_ref, lnb_ref, w2_ref, yt_ref, fillers,
              *, pz, py, ob):
    tm, d = h.shape
    nslab = d // V7X_LANES
    width = wdw_ref.shape[0]
    first = CONF_HALO - (width - 1)
    n_blocks = tm // ob
    done = 0

    for i in range(n_blocks):
        while done * n_blocks < (i + 1) * len(fillers):
            fillers[done]()
            done += 1
        t0 = i * ob
        acc = [bdw_ref[...]] * ob
        for u in range(ob + width - 1):
            zu = z_ref[pl.ds(t0 + u + first, V7X_SUBLANES, stride=pz), :]
            for o in range(ob):
                if 0 <= u - o < width:
                    acc[o] = acc[o] + zu * wdw_ref[u - o]
        for o in range(ob):
            yt_ref[pl.ds(t0 + o, V7X_SUBLANES, stride=py), :] = acc[o]

    assert done == len(fillers)
    y = jnp.concatenate([yt_ref[j * py:j * py + tm] for j in range(nslab)], axis=1)
    mu = jnp.mean(y, axis=-1, keepdims=True)
    yc = y - mu
    yn = yc * lax.rsqrt(jnp.mean(yc * yc, axis=-1, keepdims=True) + EPS)
    yn = yn * lng_ref[...] + lnb_ref[...]
    return h + _dot(jax.nn.silu(yn).astype(BF16), w2_ref[...])


def _conf_kernel(hc_ref, hx_ref, g_ref, w1_ref, wdw_ref, bdw_ref, lng_ref, lnb_ref, w2_ref,
                 o_ref, hn_ref, z0_ref, z1_ref, yt_ref, *, pz, py, ob):
    tm, d = hx_ref.shape
    nslab = d // V7X_LANES
    steps = functools.partial(_conf_project_steps, g_ref=g_ref, w1_ref=w1_ref, hn_ref=hn_ref,
                              pz=pz, col_block=2 * V7X_LANES, row_block=tm // 2)
    mix = functools.partial(_conf_mix, wdw_ref=wdw_ref, bdw_ref=bdw_ref, lng_ref=lng_ref,
                            lnb_ref=lnb_ref, w2_ref=w2_ref, yt_ref=yt_ref, pz=pz, py=py, ob=ob)

    def set_halo(dst_ref, src_ref):
        for j in range(nslab):
            rows = slice(j * pz, j * pz + CONF_HALO)
            if src_ref is None:
                dst_ref[rows] = jnp.zeros((CONF_HALO, V7X_LANES), F32)
            else:
                dst_ref[rows] = src_ref[j * pz + tm:j * pz + tm + CONF_HALO]

    @pl.when((pl.program_id(0) == 0) & (pl.program_id(1) == 0))
    def _():
        for step in steps(lambda r: hc_ref[r, :], z_ref=z0_ref):
            step()

    @pl.when(pl.program_id(1) == 0)
    def _():
        set_halo(z0_ref, None)

    @pl.when(pl.program_id(1) > 0)
    def _():
        set_halo(z0_ref, z1_ref)

    o_ref[0:tm] = mix(hc_ref[0:tm], z0_ref,
                      fillers=steps(lambda r: hc_ref[tm + r.start:tm + r.stop, :], z_ref=z1_ref))
    set_halo(z1_ref, z0_ref)
    o_ref[tm:2 * tm] = mix(hc_ref[tm:2 * tm], z1_ref,
                           fillers=steps(lambda r: hx_ref[r, :], z_ref=z0_ref))


def _slab_pitch(rows):
    p = -(-rows // V7X_SUBLANES)
    return V7X_SUBLANES * (p + 1 - p % 2)


def _conf_layer(h, g, w_pw1_all, w_dw, b_dw, ln_g, ln_b, w_pw2_all, *, layer, tm=512, ob=8):
    bsz, t, d = h.shape
    nslab = d // V7X_LANES
    pz, py = _slab_pitch(tm + CONF_HALO), _slab_pitch(tm)
    small = (g.reshape(1, d), w_dw.reshape(-1, nslab, V7X_LANES), b_dw.reshape(nslab, V7X_LANES),
             ln_g.reshape(1, d), ln_b.reshape(1, d))
    args = (small[0], w_pw1_all) + small[1:] + (w_pw2_all,)
    specs = ([_full_spec(small[0]), _layer_spec(w_pw1_all, layer)]
             + [_full_spec(a) for a in small[1:]] + [_layer_spec(w_pw2_all, layer)])
    cur, nxt = _pair_specs(bsz, t, d, tm)
    zbuf = pltpu.VMEM((nslab * pz, V7X_LANES), F32)
    return pl.pallas_call(
        functools.partial(_conf_kernel, pz=pz, py=py, ob=ob),
        grid=(bsz, t // (2 * tm)),
        in_specs=[cur, nxt] + specs,
        out_specs=cur,
        out_shape=jax.ShapeDtypeStruct(h.shape, F32),
        scratch_shapes=[pltpu.VMEM((tm, d), BF16), zbuf, zbuf,
                        pltpu.VMEM((nslab * py, V7X_LANES), F32)],
        compiler_params=pltpu.CompilerParams(
            dimension_semantics=("arbitrary", "arbitrary"),
            vmem_limit_bytes=V7X_VMEM_LIMIT_BYTES),
        name="conformer",
    )(h, h, *args)


def _gla_project_steps(read_h, g_ref, win_ref, wgk_ref, bgk_ref, hn_ref, proj_ref, lg_ref,
                       *, r0, col_block, row_block):
    tm = hn_ref.shape[0]
    n_total = win_ref.shape[1]

    def norm():
        hn_ref[...] = _rms(read_h(), g_ref[...]).astype(BF16)

    def project(c0, c1):
        def step():
            proj_ref[:, c0:c1] = _dot(hn_ref[...], win_ref[:, c0:c1])
        return step

    def gate(r):
        def step():
            gk = _dot(proj_ref[r, r0:r0 + GLA_RANK_PAD].astype(BF16), wgk_ref[...]) + bgk_ref[...]
            lg_ref[r, :] = jax.nn.log_sigmoid(gk) * (1.0 / GLA_GATE_NORM)
        return step

    return ([norm]
            + [project(c, min(c + col_block, n_total)) for c in range(0, n_total, col_block)]
            + [gate(slice(r, r + row_block)) for r in range(0, tm, row_block)])


def _gla_mix(h, proj_ref, lg_ref, gn_ref, wo_ref, st_ref, oall_ref, fillers, *, key_dim, val_dim):
    tm = h.shape[0]
    dk = key_dim // GLA_HEADS
    dv = val_dim // GLA_HEADS
    k0, v0, g0 = key_dim, 2 * key_dim, 2 * key_dim + val_dim
    row = lax.broadcasted_iota(jnp.int32, (GLA_CHUNK, GLA_CHUNK), 0)
    col = lax.broadcasted_iota(jnp.int32, (GLA_CHUNK, GLA_CHUNK), 1)
    causal = row >= col
    tril = causal.astype(BF16)
    n_units = (tm // GLA_CHUNK) * GLA_HEADS
    done = 0

    for c in range(tm // GLA_CHUNK):
        r = slice(c * GLA_CHUNK, (c + 1) * GLA_CHUNK)
        lg = lg_ref[r, :]
        lg_hi = lg.astype(BF16)
        rem = lg - lg_hi.astype(F32)
        lg_mid = rem.astype(BF16)
        lg_lo = (rem - lg_mid.astype(F32)).astype(BF16)
        b = _dot(tril, lg_hi) + _dot(tril, lg_mid) + _dot(tril, lg_lo)
        b_last = b[GLA_CHUNK - 1:GLA_CHUNK, :]
        k = proj_ref[r, k0:k0 + key_dim]
        qe = proj_ref[r, 0:key_dim] * (dk ** -0.5) * jnp.exp(b)
        ke = k * jnp.exp(-b)
        kd = k * jnp.exp(b_last - b)
        decay = jnp.exp(b_last)
        for hh in range(GLA_HEADS):
            unit = c * GLA_HEADS + hh
            while done * n_units < (unit + 1) * len(fillers):
                fillers[done]()
                done += 1
            ks = slice(hh * dk, (hh + 1) * dk)
            qeh = qe[:, ks].astype(BF16)
            v = proj_ref[r, v0 + hh * dv:v0 + (hh + 1) * dv]
            att = jnp.where(causal, _dot_nt(qeh, ke[:, ks].astype(BF16)), 0.0)
            st = st_ref[hh]
            o = _dot(att.astype(BF16), v.astype(BF16)) + _dot_nt(qeh, st.astype(BF16))
            st_ref[hh] = st * decay[:, ks] + _dot(v.T.astype(BF16), kd[:, ks].astype(BF16))
            o = o * lax.rsqrt(jnp.mean(o * o, axis=-1, keepdims=True) + EPS) * gn_ref[...]
            o = o * jax.nn.silu(proj_ref[r, g0 + hh * dv:g0 + (hh + 1) * dv])
            oall_ref[r, hh * dv:(hh + 1) * dv] = o.astype(BF16)

    assert done == len(fillers)
    return h + _dot(oall_ref[...], wo_ref[...])


def _gla_kernel(hc_ref, hx_ref, g_ref, win_ref, wgk_ref, bgk_ref, gn_ref, wo_ref, o_ref,
                hn_ref, proj0, proj1, lg0, lg1, oall0, oall1, st_ref, *, key_dim, val_dim):
    tm = hx_ref.shape[0]
    steps = functools.partial(_gla_project_steps, g_ref=g_ref, win_ref=win_ref, wgk_ref=wgk_ref,
                              bgk_ref=bgk_ref, hn_ref=hn_ref, r0=2 * key_dim + 2 * val_dim,
                              col_block=2 * V7X_LANES, row_block=GLA_CHUNK)
    mix = functools.partial(_gla_mix, gn_ref=gn_ref, wo_ref=wo_ref, st_ref=st_ref,
                            key_dim=key_dim, val_dim=val_dim)

    @pl.when((pl.program_id(0) == 0) & (pl.program_id(1) == 0))
    def _():
        for step in steps(lambda: hc_ref[0:tm], proj_ref=proj0, lg_ref=lg0):
            step()

    @pl.when(pl.program_id(1) == 0)
    def _():
        st_ref[...] = jnp.zeros_like(st_ref)

    o_ref[0:tm] = mix(hc_ref[0:tm], proj0, lg0, oall_ref=oall0,
                      fillers=steps(lambda: hc_ref[tm:2 * tm], proj_ref=proj1, lg_ref=lg1))
    o_ref[tm:2 * tm] = mix(hc_ref[tm:2 * tm], proj1, lg1, oall_ref=oall1,
                           fillers=steps(lambda: hx_ref[...], proj_ref=proj0, lg_ref=lg0))


def _gla_layer(h, g, w_in_all, w_gk_all, b_gk, g_norm, w_o_all, *, layer, tm=256):
    bsz, t, d = h.shape
    key_dim = w_gk_all.shape[2]
    val_dim = w_o_all.shape[1]
    n_main = 2 * key_dim + 2 * val_dim
    args = (g.reshape(1, d), w_in_all, w_gk_all, b_gk.reshape(1, key_dim),
            g_norm.reshape(1, -1), w_o_all)
    specs = [_layer_spec(a, layer) if a.ndim == 3 else _full_spec(a) for a in args]
    cur, nxt = _pair_specs(bsz, t, d, tm)
    proj = pltpu.VMEM((tm, n_main + GLA_RANK_PAD), F32)
    lg = pltpu.VMEM((tm, key_dim), F32)
    oall = pltpu.VMEM((tm, val_dim), BF16)
    return pl.pallas_call(
        functools.partial(_gla_kernel, key_dim=key_dim, val_dim=val_dim),
        grid=(bsz, t // (2 * tm)),
        in_specs=[cur, nxt] + specs,
        out_specs=cur,
        out_shape=jax.ShapeDtypeStruct(h.shape, F32),
        scratch_shapes=[pltpu.VMEM((tm, d), BF16), proj, proj, lg, lg, oall, oall,
                        pltpu.VMEM((GLA_HEADS, val_dim // GLA_HEADS, key_dim // GLA_HEADS), F32)],
        compiler_params=pltpu.CompilerParams(
            dimension_semantics=("arbitrary", "arbitrary"),
            vmem_limit_bytes=V7X_VMEM_LIMIT_BYTES),
        name="gla",
    )(h, h, *args)


def kernel(x, gla_w_in, gla_w_gk2, gla_b_gk, gla_norm, gla_w_o, cm_w_pw1, cm_w_dw, cm_b_dw, cm_ln_g, cm_ln_b, cm_w_pw2, ffn_w_up, ffn_w_dw, ffn_b_dw, ffn_w_down, norm_mix, norm_ffn, norm_final):
    depth = norm_mix.shape[0]
    rank_pad = GLA_RANK_PAD - gla_w_gk2.shape[1]
    gla_in = jnp.pad(gla_w_in.astype(BF16), ((0, 0), (0, 0), (0, rank_pad)))
    gla_gk = jnp.pad(gla_w_gk2.astype(BF16), ((0, 0), (0, rank_pad), (0, 0)))
    gla_o = gla_w_o.astype(BF16)
    cm_pw1, cm_pw2 = cm_w_pw1.astype(BF16), cm_w_pw2.astype(BF16)
    ffn_up, ffn_down = ffn_w_up.astype(BF16), ffn_w_down.astype(BF16)
    h = x
    for i in range(depth):
        j = i // 2
        if i % 2 == 0:
            h = _gla_layer(h, norm_mix[i], gla_in, gla_gk, gla_b_gk[j], gla_norm[j], gla_o,
                           layer=j)
        else:
            h = _conf_layer(h, norm_mix[i], cm_pw1, cm_w_dw[j], cm_b_dw[j],
                            cm_ln_g[j], cm_ln_b[j], cm_pw2, layer=j)
        h = _ffn_layer(h, norm_ffn[i], ffn_up, ffn_w_dw[i], ffn_b_dw[i], ffn_down, norm_final,
                       layer=i, final_norm=(i == depth - 1))
    return h
```
